```python
import jax, jax.numpy as jnp
from jax import lax
import numpy as np

D_MODEL = 2048
BATCH = 1
SEQ = 16384
DEPTH = 1
DEC_BATCH = 32
DEC_SEQ = 32
PAST_LEN = 4096

CHUNK = 64
N_META = 16
D_CONV = 1024
CONV_WIDTH = 31
N_HEADS = 8
HEAD_DIM = 128
D_ATTN = N_HEADS * HEAD_DIM
ATTN_SCALE = HEAD_DIM ** -0.5
Q_BLOCK = 128
PEER_HEADS = 8
PEER_NKEYS = 128
PEER_N_EXPERTS = PEER_NKEYS * PEER_NKEYS
PEER_DKEY = 256
PEER_TOPK = 16
PEER_BLOCK = 128
FORGET_BIAS_MEAN = 1.0
EPS = 1e-6

OFF_Q = 2 * D_CONV
OFF_K = OFF_Q + D_ATTN
OFF_V = OFF_K + D_ATTN
OFF_F = OFF_V + D_ATTN
OFF_G = OFF_F + N_HEADS
D_IN = OFF_G + 2 * D_MODEL

kernel_name = 'hybrid_conformer_fox_peer_stream_step'


def rmsnorm(x, g):
    x32 = x.astype(jnp.float32)
    y = x32 * lax.rsqrt(jnp.mean(x32 * x32, axis=-1, keepdims=True) + EPS)
    return (y * g.astype(jnp.float32)).astype(x.dtype)


def conformer_conv(u_ext, conv_w, conv_b, g_norm, w_conv_out):
    y = lax.conv_general_dilated(
        u_ext, conv_w[:, None, :].astype(u_ext.dtype), window_strides=(1,), padding='VALID',
        dimension_numbers=('NWC', 'WIO', 'NWC'), feature_group_count=D_CONV)
    y = y + conv_b.astype(y.dtype)
    y = jax.nn.silu(rmsnorm(y, g_norm))
    return jnp.einsum('bsc,cd->bsd', y, w_conv_out)


def attend_block(q, cq, qpos, k, v, ck, kpos):
    s = jnp.einsum('bqhd,bkhd->bhqk', q, k, preferred_element_type=jnp.float32) * ATTN_SCALE
    s = s + (jnp.swapaxes(cq, 1, 2)[..., :, None] - jnp.swapaxes(ck, 1, 2)[..., None, :])
    mask = kpos[None, :] <= qpos[:, None]
    s = jnp.where(mask[None, None], s, -jnp.inf)
    p = jax.nn.softmax(s, axis=-1)
    return jnp.einsum('bhqk,bkhd->bqhd', p.astype(v.dtype), v)


def forgetting_attention(q, k_all, v_all, logf_all, n_past):
    B, S = q.shape[0], q.shape[1]
    L = k_all.shape[1]
    c_all = jnp.cumsum(logf_all, axis=1)
    cq = c_all[:, n_past:]
    kpos = jnp.arange(L)
    qpos = n_past + jnp.arange(S)
    if S <= Q_BLOCK:
        return attend_block(q, cq, qpos, k_all, v_all, c_all, kpos)
    nb = S // Q_BLOCK
    qb = jnp.moveaxis(q.reshape(B, nb, Q_BLOCK, N_HEADS, HEAD_DIM), 1, 0)
    cqb = jnp.moveaxis(cq.reshape(B, nb, Q_BLOCK, N_HEADS), 1, 0)
    pb = qpos.reshape(nb, Q_BLOCK)
    out = lax.map(lambda a: attend_block(a[0], a[1], a[2], k_all, v_all, c_all, kpos), (qb, cqb, pb))
    return jnp.moveaxis(out, 0, 1).reshape(B, S, N_HEADS, HEAD_DIM)


def peer(xn, w_pq, sub_keys, u_tab, v_tab):
    B, S, D = xn.shape
    n = B * S
    nb = -(-n // PEER_BLOCK)
    xf = jnp.pad(xn.reshape(n, D), ((0, nb * PEER_BLOCK - n), (0, 0)))

    def block(xb):
        q = (xb @ w_pq).reshape(PEER_BLOCK, PEER_HEADS, 2, PEER_DKEY // 2)
        s = jnp.einsum('thpc,hpnc->thpn', q, sub_keys, preferred_element_type=jnp.float32)
        sv, si = lax.top_k(s, PEER_TOPK)
        cand = sv[..., 0, :, None] + sv[..., 1, None, :]
        cand_idx = si[..., 0, :, None] * PEER_NKEYS + si[..., 1, None, :]
        top_s, top_pos = lax.top_k(cand.reshape(PEER_BLOCK, PEER_HEADS, PEER_TOPK * PEER_TOPK), PEER_TOPK)
        idx = jnp.take_along_axis(cand_idx.reshape(PEER_BLOCK, PEER_HEADS, -1), top_pos, axis=-1)
        gate = jax.nn.softmax(top_s, axis=-1)
        act = jax.nn.gelu(jnp.einsum('thkd,td->thk', u_tab[idx], xb))
        w = (gate.astype(act.dtype) * act).astype(xb.dtype)
        return jnp.einsum('thk,thkd->td', w, v_tab[idx])

    out = lax.map(block, xf.reshape(nb, PEER_BLOCK, D))
    return out.reshape(nb * PEER_BLOCK, D)[:n].reshape(B, S, D)


def layer(h, p, conv_prefix, past, n_valid):
    B, S, _ = h.shape
    xn = rmsnorm(h, p['g_mix'])
    z = jnp.einsum('bsd,de->bse', xn, p['w_in'])
    u = z[..., :D_CONV] * jax.nn.sigmoid(z[..., D_CONV:OFF_Q])
    q = rmsnorm(z[..., OFF_Q:OFF_K].reshape(B, S, N_HEADS, HEAD_DIM), p['q_gain'])
    k = rmsnorm(z[..., OFF_K:OFF_V].reshape(B, S, N_HEADS, HEAD_DIM), p['k_gain'])
    v = z[..., OFF_V:OFF_F].reshape(B, S, N_HEADS, HEAD_DIM)
    logf = jax.nn.log_sigmoid((z[..., OFF_F:OFF_G] + p['b_forget'].astype(z.dtype)).astype(jnp.float32))
    g_conv = jax.nn.sigmoid(z[..., OFF_G:OFF_G + D_MODEL])
    g_att = jax.nn.sigmoid(z[..., OFF_G + D_MODEL:])
    u_ext = jnp.concatenate([conv_prefix.astype(u.dtype), u], axis=1)
    conv_out = conformer_conv(u_ext, p['conv_w'], p['conv_b'], p['g_conv_norm'], p['w_conv_out'])
    new_conv = u_ext[:, n_valid:n_valid + CONV_WIDTH - 1]
    if past is None:
        k_all, v_all, logf_all, n_past = k, v, logf, 0
    else:
        pk, pv, plf = past
        k_all = jnp.concatenate([pk.astype(k.dtype), k], axis=1)
        v_all = jnp.concatenate([pv.astype(v.dtype), v], axis=1)
        logf_all = jnp.concatenate([plf.astype(jnp.float32), logf], axis=1)
        n_past = pk.shape[1]
    att = forgetting_attention(q, k_all, v_all, logf_all, n_past)
    att_out = jnp.einsum('bsc,cd->bsd', att.reshape(B, S, D_ATTN), p['w_att_out'])
    merged = g_conv * conv_out + g_att * att_out
    h = h + jnp.einsum('bsd,de->bse', merged, p['w_out'])
    h = h + peer(rmsnorm(h, p['g_ffn']), p['w_peer_q'], p['peer_sub_keys'], p['peer_u'], p['peer_v'])
    return h, new_conv, k, v, logf


def setup_inputs(seed: int = 0) -> dict:
    key = jax.random.key(seed)
    ks = jax.random.split(key, 24)
    nrm = lambda k, shape, scale: jax.random.normal(k, shape, jnp.float32) * scale
    return {
        'x_prompt': nrm(ks[0], (BATCH, SEQ, D_MODEL), 1.0),
        'x_sample': nrm(ks[1], (DEC_BATCH, DEC_SEQ, D_MODEL), 1.0),
        'cache_k': nrm(ks[2], (DEPTH, DEC_BATCH, PAST_LEN, N_HEADS, HEAD_DIM), 1.0),
        'cache_v': nrm(ks[3], (DEPTH, DEC_BATCH, PAST_LEN, N_HEADS, HEAD_DIM), 1.0),
        'cache_logf': jax.nn.log_sigmoid(FORGET_BIAS_MEAN + nrm(ks[4], (DEPTH, DEC_BATCH, PAST_LEN, N_HEADS), 1.0)),
        'state_conv': nrm(ks[5], (DEPTH, DEC_BATCH, CONV_WIDTH - 1, D_CONV), 0.5),
        'meta_tokens': nrm(ks[6], (N_META, D_MODEL), 1.0),
        'w_in': nrm(ks[7], (DEPTH, D_MODEL, D_IN), D_MODEL ** -0.5),
        'g_mix': 1.0 + nrm(ks[8], (DEPTH, D_MODEL), 0.05),
        'q_gain': 1.0 + nrm(ks[9], (DEPTH, N_HEADS, HEAD_DIM), 0.05),
        'k_gain': 1.0 + nrm(ks[10], (DEPTH, N_HEADS, HEAD_DIM), 0.05),
        'b_forget': FORGET_BIAS_MEAN + nrm(ks[11], (DEPTH, N_HEADS), 0.3),
        'conv_w': nrm(ks[12], (DEPTH, CONV_WIDTH, D_CONV), CONV_WIDTH ** -0.5),
        'conv_b': nrm(ks[13], (DEPTH, D_CONV), 0.02),
        'g_conv_norm': 1.0 + nrm(ks[14], (DEPTH, D_CONV), 0.05),
        'w_conv_out': nrm(ks[15], (DEPTH, D_CONV, D_MODEL), D_CONV ** -0.5),
        'w_att_out': nrm(ks[16], (DEPTH, D_ATTN, D_MODEL), D_ATTN ** -0.5),
        'w_out': nrm(ks[17], (DEPTH, D_MODEL, D_MODEL), D_MODEL ** -0.5),
        'g_ffn': 1.0 + nrm(ks[18], (DEPTH, D_MODEL), 0.05),
        'w_peer_q': nrm(ks[19], (DEPTH, D_MODEL, PEER_HEADS * PEER_DKEY), D_MODEL ** -0.5),
        'peer_sub_keys': nrm(ks[20], (DEPTH, PEER_HEADS, 2, PEER_NKEYS, PEER_DKEY // 2), (PEER_DKEY // 2) ** -0.5),
        'peer_u': nrm(ks[21], (DEPTH, PEER_N_EXPERTS, D_MODEL), D_MODEL ** -0.5),
        'peer_v': nrm(ks[22], (DEPTH, PEER_N_EXPERTS, D_MODEL), PEER_HEADS ** -0.5),
    }


def reference(x_prompt, x_sample, cache_k, cache_v, cache_logf, state_conv, meta_tokens,
              w_in, g_mix, q_gain, k_gain, b_forget, conv_w, conv_b, g_conv_norm, w_conv_out,
              w_att_out, w_out, g_ffn, w_peer_q, peer_sub_keys, peer_u, peer_v):
    B = x_prompt.shape[0]
    S_tot = N_META + x_prompt.shape[1]
    S_pad = -(-S_tot // Q_BLOCK) * Q_BLOCK
    meta = jnp.broadcast_to(meta_tokens.astype(x_prompt.dtype)[None], (B, N_META, D_MODEL))
    hp = jnp.concatenate([meta, x_prompt], axis=1)
    hp = jnp.pad(hp, ((0, 0), (0, S_pad - S_tot), (0, 0)))
    hs = x_sample
    T = x_sample.shape[1]
    kp_l, vp_l, lfp_l, cp_l, ks_l, vs_l, lfs_l, cs_l = [], [], [], [], [], [], [], []
    for l in range(DEPTH):
        p = {'w_in': w_in[l], 'g_mix': g_mix[l], 'q_gain': q_gain[l], 'k_gain': k_gain[l],
             'b_forget': b_forget[l], 'conv_w': conv_w[l], 'conv_b': conv_b[l],
             'g_conv_norm': g_conv_norm[l], 'w_conv_out': w_conv_out[l], 'w_att_out': w_att_out[l],
             'w_out': w_out[l], 'g_ffn': g_ffn[l], 'w_peer_q': w_peer_q[l],
             'peer_sub_keys': peer_sub_keys[l], 'peer_u': peer_u[l], 'peer_v': peer_v[l]}
        zero_prefix = jnp.zeros((B, CONV_WIDTH - 1, D_CONV), hp.dtype)
        hp, cp, kp, vp, lfp = layer(hp, p, zero_prefix, None, S_tot)
        hs, cs, ksn, vsn, lfs = layer(hs, p, state_conv[l], (cache_k[l], cache_v[l], cache_logf[l]), T)
        kp_l.append(kp[:, :S_tot]); vp_l.append(vp[:, :S_tot]); lfp_l.append(lfp[:, :S_tot]); cp_l.append(cp)
        ks_l.append(ksn); vs_l.append(vsn); lfs_l.append(lfs); cs_l.append(cs)
    y_prompt = hp[:, N_META:S_tot]
    y_sample = hs
    return (y_prompt, y_sample,
            jnp.stack(kp_l), jnp.stack(vp_l), jnp.stack(lfp_l), jnp.stack(cp_l),
            jnp.stack(ks_l), jnp.stack(vs_l), jnp.stack(lfs_l), jnp.stack(cs_l))
```

```python
import functools

import jax
import jax.numpy as jnp
from jax import lax
from jax.experimental import pallas as pl
from jax.experimental.pallas import tpu as pltpu

F32 = jnp.float32
BF16 = jnp.bfloat16

N_META = 16
D_CONV = 1024
CONV_WIDTH = 31
N_HEADS = 8
HEAD_DIM = 128
D_ATTN = N_HEADS * HEAD_DIM
ATTN_SCALE = HEAD_DIM ** -0.5
PEER_HEADS = 8
PEER_NKEYS = 128
PEER_TOPK = 16
EPS = 1e-6
LANES = 128
NEG = -1e30
VMEM_LIMIT = 56 * 1024 * 1024


def _cparams(sem):
    return pltpu.CompilerParams(dimension_semantics=sem, vmem_limit_bytes=VMEM_LIMIT)


def _sigmoid(x):
    return 1.0 / (1.0 + jnp.exp(-x))


def _dot(a, b):
    return jnp.dot(a, b, preferred_element_type=F32)


def _dot_nt(a, b):
    return lax.dot_general(a, b, (((1,), (1,)), ((), ())), preferred_element_type=F32)


def _inproj_kernel(h_ref, g_ref, wf_ref, w_ref, z_ref, zf_ref, xn_ref):
    @pl.when(pl.program_id(1) == 0)
    def _():
        x = h_ref[...]
        r = lax.rsqrt(jnp.mean(x * x, axis=-1, keepdims=True) + EPS)
        xn = (x * r * g_ref[...]).astype(BF16)
        xn_ref[...] = xn
        zf_ref[...] = _dot(xn, wf_ref[...])

    z_ref[...] = _dot(xn_ref[...], w_ref[...])


def _inproj(h_all, g_mix, w_main, w_f, tm, tn):
    t_all, d = h_all.shape
    n_main = w_main.shape[1]
    return pl.pallas_call(
        _inproj_kernel,
        grid=(t_all // tm, n_main // tn),
        in_specs=[
            pl.BlockSpec((tm, d), lambda i, j: (i, 0)),
            pl.BlockSpec((1, d), lambda i, j: (0, 0)),
            pl.BlockSpec((d, LANES), lambda i, j: (0, 0)),
            pl.BlockSpec((d, tn), lambda i, j: (0, j)),
        ],
        out_specs=[
            pl.BlockSpec((tm, tn), lambda i, j: (i, j)),
            pl.BlockSpec((tm, LANES), lambda i, j: (i, 0)),
        ],
        out_shape=[
            jax.ShapeDtypeStruct((t_all, n_main), F32),
            jax.ShapeDtypeStruct((t_all, LANES), F32),
        ],
        scratch_shapes=[pltpu.VMEM((tm, d), BF16)],
        compiler_params=_cparams(("parallel", "arbitrary")),
        name="inproj",
    )(h_all, g_mix, w_f, w_main)


def _qkv_kernel(zq_ref, zk_ref, zv_ref, zf_ref, qg_ref, kg_ref, bf_ref,
                q_ref, k_ref, v_ref, kb_ref, vb_ref, lf_ref):
    for h in range(N_HEADS):
        sl = slice(h * HEAD_DIM, (h + 1) * HEAD_DIM)
        xq = zq_ref[:, sl]
        rq = lax.rsqrt(jnp.mean(xq * xq, axis=-1, keepdims=True) + EPS)
        q_ref[:, sl] = (xq * rq * qg_ref[:, sl] * ATTN_SCALE).astype(BF16)
        xk = zk_ref[:, sl]
        rk = lax.rsqrt(jnp.mean(xk * xk, axis=-1, keepdims=True) + EPS)
        kk = xk * rk * kg_ref[:, sl]
        k_ref[:, sl] = kk
        kb_ref[:, sl] = kk.astype(BF16)
    v = zv_ref[...]
    v_ref[...] = v
    vb_ref[...] = v.astype(BF16)
    x = zf_ref[...] + bf_ref[...]
    lf_ref[...] = jnp.minimum(x, 0.0) - jnp.log1p(jnp.exp(-jnp.abs(x)))


def _qkv_post(z, zf, q_gain, k_gain, b_f, tm, col_q):
    t_all = z.shape[0]
    blk = lambda c: pl.BlockSpec((tm, D_ATTN), lambda i, c=c: (i, c))
    row = lambda n: pl.BlockSpec((1, n), lambda i: (0, 0))
    out = lambda n: pl.BlockSpec((tm, n), lambda i: (i, 0))
    return pl.pallas_call(
        _qkv_kernel,
        grid=(t_all // tm,),
        in_specs=[blk(col_q), blk(col_q + 1), blk(col_q + 2), out(LANES),
                  row(D_ATTN), row(D_ATTN), row(LANES)],
        out_specs=[out(D_ATTN)] * 5 + [out(LANES)],
        out_shape=[
            jax.ShapeDtypeStruct((t_all, D_ATTN), BF16),
            jax.ShapeDtypeStruct((t_all, D_ATTN), F32),
            jax.ShapeDtypeStruct((t_all, D_ATTN), F32),
            jax.ShapeDtypeStruct((t_all, D_ATTN), BF16),
            jax.ShapeDtypeStruct((t_all, D_ATTN), BF16),
            jax.ShapeDtypeStruct((t_all, LANES), F32),
        ],
        compiler_params=_cparams(("parallel",)),
        name="qkv_post",
    )(z, z, z, zf, q_gain, k_gain, b_f)


CONV_PAD = 32
CONV_RC = 32


def _conv_kernel(za_ref, zg_ref, pre_ref, cw_ref, cb_ref, gn_ref, y_ref, st_ref, ubuf_ref,
                 *, tc, st_tile, st_end):
    i = pl.program_id(1)
    lead = CONV_PAD - (CONV_WIDTH - 1)

    @pl.when(i == 0)
    def _():
        ubuf_ref[0:CONV_PAD, :] = pre_ref[...]

    ubuf_ref[CONV_PAD:CONV_PAD + tc, :] = za_ref[...] * _sigmoid(zg_ref[...])

    for r in range(tc // CONV_RC):
        base = r * CONV_RC
        acc = jnp.broadcast_to(cb_ref[...], (CONV_RC, D_CONV))
        for w in range(CONV_WIDTH):
            acc = acc + cw_ref[w:w + 1, :] * ubuf_ref[base + lead + w:base + lead + w + CONV_RC, :]
        rn = lax.rsqrt(jnp.mean(acc * acc, axis=-1, keepdims=True) + EPS)
        y = acc * rn * gn_ref[...]
        y_ref[base:base + CONV_RC, :] = (y * _sigmoid(y)).astype(BF16)

    @pl.when(i == st_tile)
    def _():
        st_ref[...] = ubuf_ref[st_end + lead:st_end + CONV_PAD, :]

    ubuf_ref[0:CONV_PAD, :] = ubuf_ref[tc:tc + CONV_PAD, :]


def _conv(z, prefix, conv_w, conv_b, g_norm, *, n_seq, seq_rows, row0, tc, n_valid, col_a):
    nt = seq_rows // tc
    blk0 = row0 // tc
    st_tile = (n_valid - 1) // tc
    st_end = n_valid - st_tile * tc
    kern = functools.partial(_conv_kernel, tc=tc, st_tile=st_tile, st_end=st_end)
    zspec = lambda c: pl.BlockSpec((tc, D_CONV), lambda b, i, c=c: (blk0 + b * nt + i, c))
    row = pl.BlockSpec((1, D_CONV), lambda b, i: (0, 0))
    return pl.pallas_call(
        kern,
        grid=(n_seq, nt),
        in_specs=[zspec(col_a), zspec(col_a + 1),
                  pl.BlockSpec((None, CONV_PAD, D_CONV), lambda b, i: (b, 0, 0)),
                  pl.BlockSpec((CONV_PAD, D_CONV), lambda b, i: (0, 0)),
                  row, row],
        out_specs=[pl.BlockSpec((tc, D_CONV), lambda b, i: (b * nt + i, 0)),
                   pl.BlockSpec((None, CONV_WIDTH - 1, D_CONV), lambda b, i: (b, 0, 0))],
        out_shape=[jax.ShapeDtypeStruct((n_seq * seq_rows, D_CONV), BF16),
                   jax.ShapeDtypeStruct((n_seq, CONV_WIDTH - 1, D_CONV), F32)],
        scratch_shapes=[pltpu.VMEM((CONV_PAD + tc, D_CONV), F32)],
        compiler_params=_cparams(("parallel", "arbitrary")),
        name="conv",
    )(z, z, prefix, conv_w, conv_b, g_norm)


def _cumsum_kernel(x_ref, o_ref, *, nchunk):
    rows = x_ref.shape[0]
    lane = lax.broadcasted_iota(jnp.int32, (rows, LANES), 1)

    def body(c, carry):
        off = pl.multiple_of(c * LANES, LANES)
        x = x_ref[:, pl.ds(off, LANES)]
        for sh in (1, 2, 4, 8, 16, 32, 64):
            x = x + jnp.where(lane >= sh, pltpu.roll(x, sh, 1), 0.0)
        x = x + carry
        o_ref[:, pl.ds(off, LANES)] = x
        return x[:, LANES - 1:LANES]

    lax.fori_loop(0, nchunk, body, jnp.zeros((rows, 1), F32))


def _cumsum_rows(x):
    rows, n = x.shape
    return pl.pallas_call(
        functools.partial(_cumsum_kernel, nchunk=n // LANES),
        out_shape=jax.ShapeDtypeStruct((rows, n), F32),
        compiler_params=pltpu.CompilerParams(vmem_limit_bytes=VMEM_LIMIT),
        name="cumsum",
    )(x)


def _flash_kernel(q_ref, k_ref, v_ref, c_ref, o_ref, *, tq):
    qi = pl.program_id(1)
    q = q_ref[...]
    q0 = pl.multiple_of(qi * tq, tq)
    c_here = c_ref[:, pl.ds(q0, LANES)][:, 0:1]

    def scores(kj):
        k0 = pl.multiple_of(kj * tq, tq)
        s = _dot_nt(q, k_ref[pl.ds(k0, tq), :])
        return s + (c_here - c_ref[:, pl.ds(k0, tq)]), k0

    def update(s, k0, carry):
        m, l, acc = carry
        m_new = jnp.maximum(m, jnp.max(s, axis=-1, keepdims=True))
        alpha = jnp.exp(m - m_new)
        p = jnp.exp(s - m_new)
        l = alpha * l + jnp.sum(p, axis=-1, keepdims=True)
        acc = alpha * acc + _dot(p.astype(BF16), v_ref[pl.ds(k0, tq), :])
        return m_new, l, acc

    def body(kj, carry):
        s, k0 = scores(kj)
        return update(s, k0, carry)

    init = (jnp.full((tq, 1), NEG, F32), jnp.zeros((tq, 1), F32), jnp.zeros((tq, HEAD_DIM), F32))
    carry = lax.fori_loop(0, qi, body, init)
    s, k0 = scores(qi)
    row = lax.broadcasted_iota(jnp.int32, (tq, tq), 0)
    col = lax.broadcasted_iota(jnp.int32, (tq, tq), 1)
    s = jnp.where(col <= row, s, NEG)
    m, l, acc = update(s, k0, carry)
    o_ref[...] = (acc / l).astype(BF16)


def _flash_prompt(q, kb, vb, c_rows, t_p, tq):
    return pl.pallas_call(
        functools.partial(_flash_kernel, tq=tq),
        grid=(N_HEADS, t_p // tq),
        in_specs=[
            pl.BlockSpec((tq, HEAD_DIM), lambda h, i: (i, h)),
            pl.BlockSpec((t_p, HEAD_DIM), lambda h, i: (0, h)),
            pl.BlockSpec((t_p, HEAD_DIM), lambda h, i: (0, h)),
            pl.BlockSpec((None, 1, t_p), lambda h, i: (h, 0, 0)),
        ],
        out_specs=pl.BlockSpec((tq, HEAD_DIM), lambda h, i: (i, h)),
        out_shape=jax.ShapeDtypeStruct((t_p, D_ATTN), BF16),
        compiler_params=_cparams(("parallel", "arbitrary")),
        name="flash_prompt",
    )(q, kb, vb, c_rows)


def _decode_kernel(q_ref, ck_ref, cv_ref, kn_ref, vn_ref, c_ref, o_ref, *, n_past, t_new):
    q = q_ref[...]
    c = c_ref[...]
    bias = c[:, n_past - 1:n_past] - c
    s_c = _dot_nt(q, ck_ref[...].astype(BF16)) + bias[:, 0:n_past]
    s_n = _dot_nt(q, kn_ref[...]) + bias[:, n_past:n_past + t_new]
    row = lax.broadcasted_iota(jnp.int32, (t_new, t_new), 0)
    col = lax.broadcasted_iota(jnp.int32, (t_new, t_new), 1)
    s_n = jnp.where(col <= row, s_n, NEG)
    m = jnp.maximum(jnp.max(s_c, axis=-1, keepdims=True), jnp.max(s_n, axis=-1, keepdims=True))
    p_c = jnp.exp(s_c - m)
    p_n = jnp.exp(s_n - m)
    l = jnp.sum(p_c, axis=-1, keepdims=True) + jnp.sum(p_n, axis=-1, keepdims=True)
    acc = _dot(p_c.astype(BF16), cv_ref[...].astype(BF16)) + _dot(p_n.astype(BF16), vn_ref[...])
    o_ref[...] = (acc / l).astype(BF16)


def _decode_attn(q, kb, vb, cache_k, cache_v, c_rows, *, n_b, t_new, n_past, row0):
    blk0 = row0 // t_new
    new = pl.BlockSpec((t_new, HEAD_DIM), lambda b, h: (blk0 + b, h))
    cache = pl.BlockSpec((None, n_past, HEAD_DIM), lambda b, h: (b, 0, h))
    l_pad = c_rows.shape[-1]
    return pl.pallas_call(
        functools.partial(_decode_kernel, n_past=n_past, t_new=t_new),
        grid=(n_b, N_HEADS),
        in_specs=[new, cache, cache, new, new,
                  pl.BlockSpec((None, 1, l_pad), lambda b, h: (b * N_HEADS + h, 0, 0))],
        out_specs=pl.BlockSpec((t_new, HEAD_DIM), lambda b, h: (b, h)),
        out_shape=jax.ShapeDtypeStruct((n_b * t_new, D_ATTN), BF16),
        compiler_params=_cparams(("parallel", "parallel")),
        name="decode_attn",
    )(q, cache_k, cache_v, kb, vb, c_rows)


def _merge_kernel(h_ref, yc_ref, att_ref, gc_ref, ga_ref, wco_ref, wao_ref, wo_ref, o_ref):
    conv_out = _dot(yc_ref[...], wco_ref[...])
    att_out = _dot(att_ref[...], wao_ref[...])
    merged = _sigmoid(gc_ref[...]) * conv_out + _sigmoid(ga_ref[...]) * att_out
    o_ref[...] = h_ref[...] + _dot(merged.astype(BF16), wo_ref[...])


def _merge(h_all, yc, att, z, w_co, w_ao, w_o, tm):
    t_all, d = h_all.shape
    const = lambda a: pl.BlockSpec(a.shape, lambda i: (0, 0), pipeline_mode=pl.Buffered(1))
    return pl.pallas_call(
        _merge_kernel,
        grid=(t_all // tm,),
        in_specs=[
            pl.BlockSpec((tm, d), lambda i: (i, 0)),
            pl.BlockSpec((tm, D_CONV), lambda i: (i, 0)),
            pl.BlockSpec((tm, D_ATTN), lambda i: (i, 0)),
            pl.BlockSpec((tm, d), lambda i: (i, 0)),
            pl.BlockSpec((tm, d), lambda i: (i, 1)),
            const(w_co), const(w_ao), const(w_o),
        ],
        out_specs=pl.BlockSpec((tm, d), lambda i: (i, 0)),
        out_shape=jax.ShapeDtypeStruct((t_all, d), F32),
        compiler_params=_cparams(("parallel",)),
        name="merge",
    )(h_all, yc, att, z, z, w_co, w_ao, w_o)


_CAND_ROWS = 16 + 7 * 8 + 8


def _route_kernel(h_ref, g_ref, wq_ref, keys_ref, xn_ref, s1_ref, s2_ref, e1_ref, e2_ref, tau_ref,
                  s_scr, a_scr, cand_scr, *, tp):
    xt = h_ref[...].T
    r = lax.rsqrt(jnp.mean(xt * xt, axis=0, keepdims=True) + EPS)
    xn = (xt * r * g_ref[...]).astype(BF16)
    xn_ref[...] = xn
    qt = _dot(wq_ref[...], xn).astype(BF16)
    neg_inf = -jnp.inf

    for hp in range(2 * PEER_HEADS):
        s_scr[hp] = _dot(keys_ref[hp], qt[hp * PEER_NKEYS:(hp + 1) * PEER_NKEYS, :])

    def topk_half(hp, _):
        work = s_scr[hp]
        for k in range(PEER_TOPK):
            mx = jnp.max(work, axis=0, keepdims=True)
            a_scr[hp, k:k + 1, :] = mx
            work = jnp.where(work == mx, neg_inf, work)
        return 0

    lax.fori_loop(0, 2 * PEER_HEADS, topk_half, 0)

    row8 = lax.broadcasted_iota(jnp.int32, (8, tp), 0)

    def head(h, _):
        a1 = a_scr[2 * h]
        a2 = a_scr[2 * h + 1]
        cand_scr[0:16, :] = a1[0:1, :] + a2
        for k in range(1, 8):
            lim = PEER_TOPK // (k + 1)
            cand_scr[8 + 8 * k:16 + 8 * k, :] = jnp.where(row8 < lim, a1[k:k + 1, :] + a2[0:8, :], neg_inf)
        cand_scr[72:80, :] = a1[8:16, :] + a2[0:1, :]
        work = cand_scr[...]
        top = a1[0:1, :] + a2[0:1, :]
        zsum = jnp.zeros((1, tp), F32)
        for k in range(PEER_TOPK):
            tau = jnp.max(work, axis=0, keepdims=True)
            work = jnp.where(work == tau, neg_inf, work)
            zsum = zsum + jnp.exp(tau - top)
        s1 = s_scr[2 * h]
        s2 = s_scr[2 * h + 1]
        s1_ref[h] = s1
        s2_ref[h] = s2
        tau_ref[pl.ds(h, 1), :] = tau
        e1_ref[h] = jnp.exp(s1 - a1[0:1, :]) / zsum
        e2_ref[h] = jnp.exp(s2 - a2[0:1, :])
        return 0

    lax.fori_loop(0, PEER_HEADS, head, 0)


def _route(h1, g_col, wq_t, keys, tp):
    t_all, d = h1.shape
    nq = wq_t.shape[0]
    const2 = lambda a: pl.BlockSpec(a.shape, lambda i: (0, 0), pipeline_mode=pl.Buffered(1))
    hspec = pl.BlockSpec((PEER_HEADS, PEER_NKEYS, tp), lambda i: (0, 0, i))
    hshape = jax.ShapeDtypeStruct((PEER_HEADS, PEER_NKEYS, t_all), F32)
    return pl.pallas_call(
        functools.partial(_route_kernel, tp=tp),
        grid=(t_all // tp,),
        in_specs=[
            pl.BlockSpec((tp, d), lambda i: (i, 0)),
            const2(g_col), const2(wq_t),
            pl.BlockSpec(keys.shape, lambda i: (0, 0, 0), pipeline_mode=pl.Buffered(1)),
        ],
        out_specs=[pl.BlockSpec((d, tp), lambda i: (0, i)), hspec, hspec, hspec, hspec,
                   pl.BlockSpec((PEER_HEADS, tp), lambda i: (0, i))],
        out_shape=[jax.ShapeDtypeStruct((d, t_all), BF16), hshape, hshape, hshape, hshape,
                   jax.ShapeDtypeStruct((PEER_HEADS, t_all), F32)],
        scratch_shapes=[
            pltpu.VMEM((2 * PEER_HEADS, PEER_NKEYS, tp), F32),
            pltpu.VMEM((2 * PEER_HEADS, PEER_TOPK, tp), F32),
            pltpu.VMEM((_CAND_ROWS, tp), F32),
        ],
        compiler_params=_cparams(("parallel",)),
        name="peer_route",
    )(h1, g_col, wq_t, keys)


def _gelu_tanh(x):
    return 0.5 * x * (1.0 + jnp.tanh(0.7978845608028654 * (x + 0.044715 * (x * x * x))))


def _expert_kernel(xn_ref, u_ref, vt_ref, s1_ref, s2_ref, e1_ref, e2_ref, tau_ref, h_ref, o_ref,
                   acc_ref, act_ref, w_ref, *, tp, eb):
    j = pl.program_id(1)
    n_i = eb // PEER_NKEYS
    n_lb = tp // LANES

    @pl.when(j == 0)
    def _():
        acc_ref[...] = jnp.zeros_like(acc_ref)

    act_ref[...] = _dot(u_ref[...], xn_ref[...])

    i0 = pl.multiple_of(j * n_i, n_i)

    def block(lb, _):
        l0 = pl.multiple_of(lb * LANES, LANES)
        lanes = pl.ds(l0, LANES)
        taus = [tau_ref[h:h + 1, lanes] for h in range(PEER_HEADS)]
        s1g = [s1_ref[h, pl.ds(i0, n_i), lanes] for h in range(PEER_HEADS)]
        e1g = [e1_ref[h, pl.ds(i0, n_i), lanes] for h in range(PEER_HEADS)]
        for ii in range(n_i):
            rows = slice(ii * PEER_NKEYS, (ii + 1) * PEER_NKEYS)
            gate = jnp.zeros((PEER_NKEYS, LANES), F32)
            for h in range(PEER_HEADS):
                sel = (s1g[h][ii:ii + 1, :] + s2_ref[h, :, lanes]) >= taus[h]
                gate = gate + jnp.where(sel, e1g[h][ii:ii + 1, :] * e2_ref[h, :, lanes], 0.0)
            w_ref[rows, lanes] = (_gelu_tanh(act_ref[rows, lanes]) * gate).astype(BF16)
        return 0

    lax.fori_loop(0, n_lb, block, 0)
    acc_ref[...] += _dot(vt_ref[...], w_ref[...])

    @pl.when(j == pl.num_programs(1) - 1)
    def _():
        o_ref[...] = h_ref[...] + acc_ref[...].T


def _experts(xn_t, u_b, vt_b, s1, s2, e1, e2, tau, h1, tp, eb):
    d, t_all = xn_t.shape
    n_e = u_b.shape[0]
    once = lambda shape, imap: pl.BlockSpec(shape, imap, pipeline_mode=pl.Buffered(1))
    hspec = once((PEER_HEADS, PEER_NKEYS, tp), lambda i, j: (0, 0, i))
    return pl.pallas_call(
        functools.partial(_expert_kernel, tp=tp, eb=eb),
        grid=(t_all // tp, n_e // eb),
        in_specs=[
            once((d, tp), lambda i, j: (0, i)),
            pl.BlockSpec((eb, d), lambda i, j: (j, 0)),
            pl.BlockSpec((d, eb), lambda i, j: (0, j)),
            hspec, hspec, hspec, hspec,
            once((PEER_HEADS, tp), lambda i, j: (0, i)),
            once((tp, d), lambda i, j: (i, 0)),
        ],
        out_specs=pl.BlockSpec((tp, d), lambda i, j: (i, 0)),
        out_shape=jax.ShapeDtypeStruct((t_all, d), F32),
        scratch_shapes=[
            pltpu.VMEM((d, tp), F32),
            pltpu.VMEM((eb, tp), F32),
            pltpu.VMEM((eb, tp), BF16),
        ],
        compiler_params=_cparams(("parallel", "arbitrary")),
        name="peer_experts",
    )(xn_t, u_b, vt_b, s1, s2, e1, e2, tau, h1)


def _round_up(x, m):
    return -(-x // m) * m


def kernel(x_prompt, x_sample, cache_k, cache_v, cache_logf, state_conv, meta_tokens, w_in, g_mix, q_gain, k_gain, b_forget, conv_w, conv_b, g_conv_norm, w_conv_out, w_att_out, w_out, g_ffn, w_peer_q, peer_sub_keys, peer_u, peer_v):
    n_bp, seq, d = x_prompt.shape
    n_bs, t_new, _ = x_sample.shape
    depth, _, n_past = cache_k.shape[:3]
    assert n_bp == 1 and depth == 1 and t_new % 16 == 0 and t_new >= CONV_WIDTH - 1
    s_tot = N_META + seq
    tm = 512 if s_tot >= 2048 else 128
    t_p = _round_up(s_tot, tm)
    t_s = n_bs * t_new
    t_all = t_p + _round_up(t_s, tm)
    tc = 128
    eb = 8 * PEER_NKEYS

    h_all = jnp.concatenate([
        meta_tokens.astype(F32), x_prompt[0], jnp.zeros((t_p - s_tot, d), F32),
        x_sample.reshape(t_s, d), jnp.zeros((t_all - t_p - t_s, d), F32)], axis=0)

    w = w_in[0]
    o_q, o_k, o_v, o_f = 2 * D_CONV, 2 * D_CONV + D_ATTN, 2 * D_CONV + 2 * D_ATTN, 2 * D_CONV + 3 * D_ATTN
    o_g = o_f + N_HEADS
    w_main = jnp.concatenate([w[:, o_g:], w[:, :o_f]], axis=1).astype(BF16)
    w_f = jnp.pad(w[:, o_f:o_g], ((0, 0), (0, LANES - N_HEADS))).astype(BF16)
    col_a = 2 * d // D_CONV
    col_q = col_a + 2
    b_f = jnp.pad(b_forget[0], (0, LANES - N_HEADS)).reshape(1, LANES)

    z, zf = _inproj(h_all, g_mix[0].reshape(1, d), w_main, w_f, tm, 1024)
    q_s, k_f, v_f, k_b, v_b, logf = _qkv_post(z, zf, q_gain[0].reshape(1, D_ATTN), k_gain[0].reshape(1, D_ATTN),
                                              b_f, tm, col_q)

    cw = jnp.pad(conv_w[0], ((0, CONV_PAD - CONV_WIDTH), (0, 0)))
    cb = conv_b[0].reshape(1, D_CONV)
    gn = g_conv_norm[0].reshape(1, D_CONV)
    lead = CONV_PAD - (CONV_WIDTH - 1)
    yc_p, conv_p = _conv(z, jnp.zeros((1, CONV_PAD, D_CONV), F32), cw, cb, gn,
                         n_seq=1, seq_rows=t_p, row0=0, tc=tc, n_valid=s_tot, col_a=col_a)
    yc_s, conv_s = _conv(z, jnp.pad(state_conv[0], ((0, 0), (lead, 0), (0, 0))), cw, cb, gn,
                         n_seq=n_bs, seq_rows=t_new, row0=t_p, tc=t_new, n_valid=t_new, col_a=col_a)

    lf8 = logf[:, :N_HEADS]
    c_p = _cumsum_rows(lf8[:t_p].T).reshape(N_HEADS, 1, t_p)
    att_p = _flash_prompt(q_s, k_b, v_b, c_p, t_p, tm)
    l_pad = _round_up(n_past + t_new, LANES)
    lf_new = lf8[t_p:t_p + t_s].reshape(n_bs, t_new, N_HEADS)
    lf_all = jnp.concatenate([cache_logf[0].astype(F32), lf_new], axis=1)
    lf_all = jnp.pad(jnp.swapaxes(lf_all, 1, 2), ((0, 0), (0, 0), (0, l_pad - n_past - t_new)))
    c_s = _cumsum_rows(lf_all.reshape(n_bs * N_HEADS, l_pad)).reshape(n_bs * N_HEADS, 1, l_pad)
    att_s = _decode_attn(q_s, k_b, v_b, cache_k[0].reshape(n_bs, n_past, D_ATTN),
                         cache_v[0].reshape(n_bs, n_past, D_ATTN), c_s,
                         n_b=n_bs, t_new=t_new, n_past=n_past, row0=t_p)

    pad_s = t_all - t_p - t_s
    yc = jnp.concatenate([yc_p, yc_s, jnp.zeros((pad_s, D_CONV), BF16)], axis=0)
    att = jnp.concatenate([att_p, att_s, jnp.zeros((pad_s, D_ATTN), BF16)], axis=0)
    h1 = _merge(h_all, yc, att, z, w_conv_out[0].astype(BF16), w_att_out[0].astype(BF16),
                w_out[0].astype(BF16), min(tm, 256))

    keys = peer_sub_keys[0].reshape(2 * PEER_HEADS, PEER_NKEYS, PEER_NKEYS).astype(BF16)
    xn_t, s1, s2, e1, e2, tau = _route(h1, g_ffn[0].reshape(d, 1), w_peer_q[0].T.astype(BF16), keys, tm)
    y = _experts(xn_t, peer_u[0].astype(BF16), peer_v[0].T.astype(BF16), s1, s2, e1, e2, tau, h1, tm, eb)

    hd = (N_HEADS, HEAD_DIM)
    return (
        y[N_META:s_tot][None],
        y[t_p:t_p + t_s].reshape(n_bs, t_new, d),
        k_f[:s_tot].reshape(1, 1, s_tot, *hd),
        v_f[:s_tot].reshape(1, 1, s_tot, *hd),
        lf8[:s_tot].reshape(1, 1, s_tot, N_HEADS),
        conv_p[None],
        k_f[t_p:t_p + t_s].reshape(1, n_bs, t_new, *hd),
        v_f[t_p:t_p + t_s].reshape(1, n_bs, t_new, *hd),
        lf_new[None],
        conv_s[None],
    )
```

```python
import functools

import jax
import jax.numpy as jnp
from jax import lax
from jax.experimental import pallas as pl
from jax.experimental.pallas import tpu as pltpu

F32 = jnp.float32
BF16 = jnp.bfloat16

N_META = 16
D_CONV = 1024
CONV_WIDTH = 31
N_HEADS = 8
HEAD_DIM = 128
D_ATTN = N_HEADS * HEAD_DIM
ATTN_SCALE = HEAD_DIM ** -0.5
PEER_HEADS = 8
PEER_NKEYS = 128
PEER_TOPK = 16
EPS = 1e-6
LOG2E = 1.4426950408889634
LANES = 128
NEG = -1e30
VMEM_LIMIT = 56 * 1024 * 1024


def _cparams(sem):
    return pltpu.CompilerParams(dimension_semantics=sem, vmem_limit_bytes=VMEM_LIMIT)


def _sigmoid(x):
    return 1.0 / (1.0 + jnp.exp(-x))


def _dot(a, b):
    return jnp.dot(a, b, preferred_element_type=F32)


def _dot_nt(a, b):
    return lax.dot_general(a, b, (((1,), (1,)), ((), ())), preferred_element_type=F32)


def _inproj_kernel(h_ref, g_ref, wf_ref, w_ref, z_ref, zf_ref, xn_ref):
    @pl.when(pl.program_id(1) == 0)
    def _():
        x = h_ref[...]
        r = lax.rsqrt(jnp.mean(x * x, axis=-1, keepdims=True) + EPS)
        xn = (x * r * g_ref[...]).astype(BF16)
        xn_ref[...] = xn
        zf_ref[...] = _dot(xn, wf_ref[...])

    z_ref[...] = _dot(xn_ref[...], w_ref[...])


def _inproj(h_all, g_mix, w_main, w_f, tm, tn):
    t_all, d = h_all.shape
    n_main = w_main.shape[1]
    return pl.pallas_call(
        _inproj_kernel,
        grid=(t_all // tm, n_main // tn),
        in_specs=[
            pl.BlockSpec((tm, d), lambda i, j: (i, 0)),
            pl.BlockSpec((1, d), lambda i, j: (0, 0)),
            pl.BlockSpec((d, LANES), lambda i, j: (0, 0)),
            pl.BlockSpec((d, tn), lambda i, j: (0, j)),
        ],
        out_specs=[
            pl.BlockSpec((tm, tn), lambda i, j: (i, j)),
            pl.BlockSpec((tm, LANES), lambda i, j: (i, 0)),
        ],
        out_shape=[
            jax.ShapeDtypeStruct((t_all, n_main), F32),
            jax.ShapeDtypeStruct((t_all, LANES), F32),
        ],
        scratch_shapes=[pltpu.VMEM((tm, d), BF16)],
        compiler_params=_cparams(("parallel", "arbitrary")),
        name="inproj",
    )(h_all, g_mix, w_f, w_main)


def _qkv_kernel(zq_ref, zk_ref, zv_ref, zf_ref, qg_ref, kg_ref, bf_ref,
                q_ref, k_ref, v_ref, kb_ref, vb_ref, vt_ref, lf_ref):
    for h in range(N_HEADS):
        sl = slice(h * HEAD_DIM, (h + 1) * HEAD_DIM)
        xq = zq_ref[:, sl]
        rq = lax.rsqrt(jnp.mean(xq * xq, axis=-1, keepdims=True) + EPS)
        q_ref[:, sl] = (xq * rq * qg_ref[:, sl] * (ATTN_SCALE * LOG2E)).astype(BF16)
        xk = zk_ref[:, sl]
        rk = lax.rsqrt(jnp.mean(xk * xk, axis=-1, keepdims=True) + EPS)
        kk = xk * rk * kg_ref[:, sl]
        k_ref[:, sl] = kk
        kb_ref[:, sl] = kk.astype(BF16)
    v = zv_ref[...]
    v_ref[...] = v
    vb_ref[...] = v.astype(BF16)
    vt_ref[...] = v.T.astype(BF16)
    x = zf_ref[...] + bf_ref[...]
    lf_ref[...] = jnp.minimum(x, 0.0) - jnp.log1p(jnp.exp(-jnp.abs(x)))


def _qkv_post(z, zf, q_gain, k_gain, b_f, tm, col_q):
    t_all = z.shape[0]
    blk = lambda c: pl.BlockSpec((tm, D_ATTN), lambda i, c=c: (i, c))
    row = lambda n: pl.BlockSpec((1, n), lambda i: (0, 0))
    out = lambda n: pl.BlockSpec((tm, n), lambda i: (i, 0))
    return pl.pallas_call(
        _qkv_kernel,
        grid=(t_all // tm,),
        in_specs=[blk(col_q), blk(col_q + 1), blk(col_q + 2), out(LANES),
                  row(D_ATTN), row(D_ATTN), row(LANES)],
        out_specs=[out(D_ATTN)] * 5 + [pl.BlockSpec((D_ATTN, tm), lambda i: (0, i)), out(LANES)],
        out_shape=[
            jax.ShapeDtypeStruct((t_all, D_ATTN), BF16),
            jax.ShapeDtypeStruct((t_all, D_ATTN), F32),
            jax.ShapeDtypeStruct((t_all, D_ATTN), F32),
            jax.ShapeDtypeStruct((t_all, D_ATTN), BF16),
            jax.ShapeDtypeStruct((t_all, D_ATTN), BF16),
            jax.ShapeDtypeStruct((D_ATTN, t_all), BF16),
            jax.ShapeDtypeStruct((t_all, LANES), F32),
        ],
        compiler_params=_cparams(("parallel",)),
        name="qkv_post",
    )(z, z, z, zf, q_gain, k_gain, b_f)


CONV_PAD = 32
CONV_RC = 32


def _conv_kernel(za_ref, zg_ref, pre_ref, cw_ref, cb_ref, gn_ref, y_ref, st_ref, ubuf_ref,
                 *, tc, st_tile, st_end):
    i = pl.program_id(1)
    lead = CONV_PAD - (CONV_WIDTH - 1)

    @pl.when(i == 0)
    def _():
        ubuf_ref[0:CONV_PAD, :] = pre_ref[...]

    ubuf_ref[CONV_PAD:CONV_PAD + tc, :] = za_ref[...] * _sigmoid(zg_ref[...])

    for r in range(tc // CONV_RC):
        base = r * CONV_RC
        acc = jnp.broadcast_to(cb_ref[...], (CONV_RC, D_CONV))
        for w in range(CONV_WIDTH):
            acc = acc + cw_ref[w:w + 1, :] * ubuf_ref[base + lead + w:base + lead + w + CONV_RC, :]
        rn = lax.rsqrt(jnp.mean(acc * acc, axis=-1, keepdims=True) + EPS)
        y = acc * rn * gn_ref[...]
        y_ref[base:base + CONV_RC, :] = (y * _sigmoid(y)).astype(BF16)

    @pl.when(i == st_tile)
    def _():
        st_ref[...] = ubuf_ref[st_end + lead:st_end + CONV_PAD, :]

    ubuf_ref[0:CONV_PAD, :] = ubuf_ref[tc:tc + CONV_PAD, :]


def _conv(z, prefix, conv_w, conv_b, g_norm, *, n_seq, seq_rows, row0, tc, n_valid, col_a):
    nt = seq_rows // tc
    blk0 = row0 // tc
    st_tile = (n_valid - 1) // tc
    st_end = n_valid - st_tile * tc
    kern = functools.partial(_conv_kernel, tc=tc, st_tile=st_tile, st_end=st_end)
    zspec = lambda c: pl.BlockSpec((tc, D_CONV), lambda b, i, c=c: (blk0 + b * nt + i, c))
    row = pl.BlockSpec((1, D_CONV), lambda b, i: (0, 0))
    return pl.pallas_call(
        kern,
        grid=(n_seq, nt),
        in_specs=[zspec(col_a), zspec(col_a + 1),
                  pl.BlockSpec((None, CONV_PAD, D_CONV), lambda b, i: (b, 0, 0)),
                  pl.BlockSpec((CONV_PAD, D_CONV), lambda b, i: (0, 0)),
                  row, row],
        out_specs=[pl.BlockSpec((tc, D_CONV), lambda b, i: (b * nt + i, 0)),
                   pl.BlockSpec((None, CONV_WIDTH - 1, D_CONV), lambda b, i: (b, 0, 0))],
        out_shape=[jax.ShapeDtypeStruct((n_seq * seq_rows, D_CONV), BF16),
                   jax.ShapeDtypeStruct((n_seq, CONV_WIDTH - 1, D_CONV), F32)],
        scratch_shapes=[pltpu.VMEM((CONV_PAD + tc, D_CONV), F32)],
        compiler_params=_cparams(("parallel", "arbitrary")),
        name="conv",
    )(z, z, prefix, conv_w, conv_b, g_norm)


def _cumsum_kernel(x_ref, o_ref, *, nchunk):
    rows = x_ref.shape[0]
    lane = lax.broadcasted_iota(jnp.int32, (rows, LANES), 1)

    def body(c, carry):
        off = pl.multiple_of(c * LANES, LANES)
        x = x_ref[:, pl.ds(off, LANES)]
        for sh in (1, 2, 4, 8, 16, 32, 64):
            x = x + jnp.where(lane >= sh, pltpu.roll(x, sh, 1), 0.0)
        x = x + carry
        o_ref[:, pl.ds(off, LANES)] = x
        return x[:, LANES - 1:LANES]

    lax.fori_loop(0, nchunk, body, jnp.zeros((rows, 1), F32))


def _cumsum_rows(x):
    rows, n = x.shape
    return pl.pallas_call(
        functools.partial(_cumsum_kernel, nchunk=n // LANES),
        out_shape=jax.ShapeDtypeStruct((rows, n), F32),
        compiler_params=pltpu.CompilerParams(vmem_limit_bytes=VMEM_LIMIT),
        name="cumsum",
    )(x)


AUG = 2 * HEAD_DIM
BIAS_PIECES = 3


def _attn_prep_kernel(q_ref, k_ref, c_ref, qa_ref, ka_ref):
    tm = q_ref.shape[0]
    lane = lax.broadcasted_iota(jnp.int32, (tm, HEAD_DIM), 1)
    ones = jnp.where(lane < BIAS_PIECES, 1.0, 0.0).astype(BF16)
    row8 = lax.broadcasted_iota(jnp.int32, (8, tm), 0)
    zeros = jnp.zeros((HEAD_DIM - 8, tm), F32)
    for h in range(N_HEADS):
        qa_ref[:, h * AUG:h * AUG + HEAD_DIM] = q_ref[:, h * HEAD_DIM:(h + 1) * HEAD_DIM]
        qa_ref[:, h * AUG + HEAD_DIM:(h + 1) * AUG] = ones
        ka_ref[:, h * AUG:h * AUG + HEAD_DIM] = k_ref[:, h * HEAD_DIM:(h + 1) * HEAD_DIM]
        b = c_ref[h:h + 1, :] * (-LOG2E)
        hi = b.astype(BF16).astype(F32)
        mid = (b - hi).astype(BF16).astype(F32)
        lo = b - hi - mid
        pieces = jnp.where(row8 == 0, hi, jnp.where(row8 == 1, mid, jnp.where(row8 == 2, lo, 0.0)))
        ka_ref[:, h * AUG + HEAD_DIM:(h + 1) * AUG] = jnp.concatenate([pieces, zeros], axis=0).T.astype(BF16)


def _attn_prep(q, kb, c_rows, t_p, tm):
    return pl.pallas_call(
        _attn_prep_kernel,
        grid=(t_p // tm,),
        in_specs=[
            pl.BlockSpec((tm, D_ATTN), lambda i: (i, 0)),
            pl.BlockSpec((tm, D_ATTN), lambda i: (i, 0)),
            pl.BlockSpec((N_HEADS, tm), lambda i: (0, i)),
        ],
        out_specs=[pl.BlockSpec((tm, N_HEADS * AUG), lambda i: (i, 0))] * 2,
        out_shape=[jax.ShapeDtypeStruct((t_p, N_HEADS * AUG), BF16)] * 2,
        compiler_params=_cparams(("parallel",)),
        name="attn_prep",
    )(q, kb, c_rows)


FLASH_HEADS = 2


def _flash_kernel(qa_ref, ka_ref, vt_ref, o_ref, m_scr, l_scr, acc_scr, *, tq):
    qi = pl.program_id(1)

    m_scr[...] = jnp.full_like(m_scr, NEG)
    l_scr[...] = jnp.zeros_like(l_scr)
    acc_scr[...] = jnp.zeros_like(acc_scr)

    def step(kj, masked):
        k0 = pl.multiple_of(kj * tq, tq)
        for hh in range(FLASH_HEADS):
            s = _dot_nt(ka_ref[pl.ds(k0, tq), hh * AUG:(hh + 1) * AUG], qa_ref[:, hh * AUG:(hh + 1) * AUG])
            if masked:
                key = lax.broadcasted_iota(jnp.int32, (tq, tq), 0)
                qry = lax.broadcasted_iota(jnp.int32, (tq, tq), 1)
                s = jnp.where(key <= qry, s, NEG)
            m_old = m_scr[hh]
            m_new = jnp.maximum(m_old, jnp.max(s, axis=0, keepdims=True))
            alpha = jnp.exp2(m_old - m_new)
            p = jnp.exp2(s - m_new)
            l_scr[hh] = alpha * l_scr[hh] + jnp.sum(p, axis=0, keepdims=True)
            pv = _dot(vt_ref[hh * HEAD_DIM:(hh + 1) * HEAD_DIM, pl.ds(k0, tq)], p.astype(BF16))
            acc_scr[hh] = alpha * acc_scr[hh] + pv
            m_scr[hh] = m_new

    def body(kj, _):
        step(kj, False)
        return 0

    lax.fori_loop(0, qi, body, 0)
    step(qi, True)
    for hh in range(FLASH_HEADS):
        o_ref[:, hh * HEAD_DIM:(hh + 1) * HEAD_DIM] = (acc_scr[hh] / l_scr[hh]).T.astype(BF16)


def _flash_prompt(qa, ka, vt, t_p, tq):
    once = lambda shape, imap: pl.BlockSpec(shape, imap, pipeline_mode=pl.Buffered(1))
    return pl.pallas_call(
        functools.partial(_flash_kernel, tq=tq),
        grid=(N_HEADS // FLASH_HEADS, t_p // tq),
        in_specs=[
            pl.BlockSpec((tq, FLASH_HEADS * AUG), lambda h, i: (i, h)),
            once((t_p, FLASH_HEADS * AUG), lambda h, i: (0, h)),
            once((FLASH_HEADS * HEAD_DIM, t_p), lambda h, i: (h, 0)),
        ],
        out_specs=pl.BlockSpec((tq, FLASH_HEADS * HEAD_DIM), lambda h, i: (i, h)),
        out_shape=jax.ShapeDtypeStruct((t_p, D_ATTN), BF16),
        scratch_shapes=[
            pltpu.VMEM((FLASH_HEADS, 1, tq), F32),
            pltpu.VMEM((FLASH_HEADS, 1, tq), F32),
            pltpu.VMEM((FLASH_HEADS, HEAD_DIM, tq), F32),
        ],
        compiler_params=_cparams(("parallel", "arbitrary")),
        name="flash_prompt",
    )(qa, ka, vt)


def _decode_kernel(q_ref, ck_ref, cv_ref, kn_ref, vn_ref, c_ref, o_ref, m_scr, l_scr, acc_scr,
                   *, n_past, t_new, ck):
    kc = pl.program_id(1)
    k0 = pl.multiple_of(kc * ck, ck)

    @pl.when(kc == 0)
    def _():
        m_scr[...] = jnp.full_like(m_scr, NEG)
        l_scr[...] = jnp.zeros_like(l_scr)
        acc_scr[...] = jnp.zeros_like(acc_scr)

    def update(h, s, v):
        m_old = m_scr[h]
        m_new = jnp.maximum(m_old, jnp.max(s, axis=-1, keepdims=True))
        alpha = jnp.exp2(m_old - m_new)
        p = jnp.exp2(s - m_new)
        l_scr[h] = alpha * l_scr[h] + jnp.sum(p, axis=-1, keepdims=True)
        acc_scr[h] = alpha * acc_scr[h] + _dot(p.astype(BF16), v)
        m_scr[h] = m_new

    for h in range(N_HEADS):
        cols = slice(h * HEAD_DIM, (h + 1) * HEAD_DIM)
        c_end = c_ref[h, :, n_past - 1:n_past]
        k_h = ck_ref[pl.ds(h, ck, stride=N_HEADS), :].astype(BF16)
        v_h = cv_ref[pl.ds(h, ck, stride=N_HEADS), :].astype(BF16)
        s = _dot_nt(q_ref[:, cols], k_h) + (c_end - c_ref[h, :, pl.ds(k0, ck)]) * LOG2E
        update(h, s, v_h)

    @pl.when(kc == pl.num_programs(1) - 1)
    def _():
        row = lax.broadcasted_iota(jnp.int32, (t_new, t_new), 0)
        col = lax.broadcasted_iota(jnp.int32, (t_new, t_new), 1)
        for h in range(N_HEADS):
            cols = slice(h * HEAD_DIM, (h + 1) * HEAD_DIM)
            c_end = c_ref[h, :, n_past - 1:n_past]
            s = _dot_nt(q_ref[:, cols], kn_ref[:, cols]) + (c_end - c_ref[h, :, n_past:n_past + t_new]) * LOG2E
            update(h, jnp.where(col <= row, s, NEG), vn_ref[:, cols])
            o_ref[:, cols] = (acc_scr[h] / l_scr[h]).astype(BF16)


def _decode_attn(q, kb, vb, cache_k, cache_v, c_rows, *, n_b, t_new, n_past, row0, ck):
    blk0 = row0 // t_new
    new = pl.BlockSpec((t_new, D_ATTN), lambda b, c: (blk0 + b, 0))
    cache = pl.BlockSpec((None, ck * N_HEADS, HEAD_DIM), lambda b, c: (b, c, 0))
    l_pad = c_rows.shape[-1]
    return pl.pallas_call(
        functools.partial(_decode_kernel, n_past=n_past, t_new=t_new, ck=ck),
        grid=(n_b, n_past // ck),
        in_specs=[new, cache, cache, new, new,
                  pl.BlockSpec((N_HEADS, 1, l_pad), lambda b, c: (b, 0, 0))],
        out_specs=pl.BlockSpec((t_new, D_ATTN), lambda b, c: (b, 0)),
        out_shape=jax.ShapeDtypeStruct((n_b * t_new, D_ATTN), BF16),
        scratch_shapes=[
            pltpu.VMEM((N_HEADS, t_new, 1), F32),
            pltpu.VMEM((N_HEADS, t_new, 1), F32),
            pltpu.VMEM((N_HEADS, t_new, HEAD_DIM), F32),
        ],
        compiler_params=_cparams(("parallel", "arbitrary")),
        name="decode_attn",
    )(q, cache_k, cache_v, kb, vb, c_rows)


def _merge_kernel(h_ref, yc_ref, att_ref, gc_ref, ga_ref, wco_ref, wao_ref, wo_ref, o_ref):
    conv_out = _dot(yc_ref[...], wco_ref[...])
    att_out = _dot(att_ref[...], wao_ref[...])
    merged = _sigmoid(gc_ref[...]) * conv_out + _sigmoid(ga_ref[...]) * att_out
    o_ref[...] = h_ref[...] + _dot(merged.astype(BF16), wo_ref[...])


def _merge(h_all, yc, att, z, w_co, w_ao, w_o, tm):
    t_all, d = h_all.shape
    const = lambda a: pl.BlockSpec(a.shape, lambda i: (0, 0), pipeline_mode=pl.Buffered(1))
    return pl.pallas_call(
        _merge_kernel,
        grid=(t_all // tm,),
        in_specs=[
            pl.BlockSpec((tm, d), lambda i: (i, 0)),
            pl.BlockSpec((tm, D_CONV), lambda i: (i, 0)),
            pl.BlockSpec((tm, D_ATTN), lambda i: (i, 0)),
            pl.BlockSpec((tm, d), lambda i: (i, 0)),
            pl.BlockSpec((tm, d), lambda i: (i, 1)),
            const(w_co), const(w_ao), const(w_o),
        ],
        out_specs=pl.BlockSpec((tm, d), lambda i: (i, 0)),
        out_shape=jax.ShapeDtypeStruct((t_all, d), F32),
        compiler_params=_cparams(("parallel",)),
        name="merge",
    )(h_all, yc, att, z, z, w_co, w_ao, w_o)


_CAND_ROWS = 16 + 7 * 8 + 8


def _route_kernel(h_ref, g_ref, wq_ref, keys_ref, xn_ref, s1_ref, s2_ref, e1_ref, e2_ref, tau_ref,
                  s_scr, a_scr, cand_scr, e1_scr, *, tp):
    xt = h_ref[...].T
    r = lax.rsqrt(jnp.mean(xt * xt, axis=0, keepdims=True) + EPS)
    xn = (xt * r * g_ref[...]).astype(BF16)
    xn_ref[...] = xn
    qt = _dot(wq_ref[...], xn).astype(BF16)
    neg_inf = -jnp.inf

    for hp in range(2 * PEER_HEADS):
        s_scr[hp] = _dot(keys_ref[hp], qt[hp * PEER_NKEYS:(hp + 1) * PEER_NKEYS, :])

    def topk_half(hp, _):
        work = s_scr[hp]
        for k in range(PEER_TOPK):
            mx = jnp.max(work, axis=0, keepdims=True)
            a_scr[hp, k:k + 1, :] = mx
            work = jnp.where(work == mx, neg_inf, work)
        return 0

    lax.fori_loop(0, 2 * PEER_HEADS, topk_half, 0)

    row8 = lax.broadcasted_iota(jnp.int32, (8, tp), 0)

    def head(h, _):
        a1 = a_scr[2 * h]
        a2 = a_scr[2 * h + 1]
        cand_scr[0:16, :] = a1[0:1, :] + a2
        for k in range(1, 8):
            lim = PEER_TOPK // (k + 1)
            cand_scr[8 + 8 * k:16 + 8 * k, :] = jnp.where(row8 < lim, a1[k:k + 1, :] + a2[0:8, :], neg_inf)
        cand_scr[72:80, :] = a1[8:16, :] + a2[0:1, :]
        work = cand_scr[...]
        top = a1[0:1, :] + a2[0:1, :]
        zsum = jnp.zeros((1, tp), F32)
        for k in range(PEER_TOPK):
            tau = jnp.max(work, axis=0, keepdims=True)
            work = jnp.where(work == tau, neg_inf, work)
            zsum = zsum + jnp.exp(tau - top)
        s1 = s_scr[2 * h]
        s2 = s_scr[2 * h + 1]
        s2_ref[h] = s2
        tau_ref[pl.ds(h, 1), :] = tau
        e1_scr[h] = jnp.exp(s1 - a1[0:1, :]) / zsum
        e2_ref[h] = jnp.exp(s2 - a2[0:1, :])
        return 0

    lax.fori_loop(0, PEER_HEADS, head, 0)
    for h in range(PEER_HEADS):
        s1_ref[:, h, :] = s_scr[2 * h]
        e1_ref[:, h, :] = e1_scr[h]


def _route(h1, g_col, wq_t, keys, tp):
    t_all, d = h1.shape
    const2 = lambda a: pl.BlockSpec(a.shape, lambda i: (0, 0), pipeline_mode=pl.Buffered(1))
    hspec = pl.BlockSpec((PEER_HEADS, PEER_NKEYS, tp), lambda i: (0, 0, i))
    hshape = jax.ShapeDtypeStruct((PEER_HEADS, PEER_NKEYS, t_all), F32)
    kspec = pl.BlockSpec((PEER_NKEYS, PEER_HEADS, tp), lambda i: (0, 0, i))
    kshape = jax.ShapeDtypeStruct((PEER_NKEYS, PEER_HEADS, t_all), F32)
    return pl.pallas_call(
        functools.partial(_route_kernel, tp=tp),
        grid=(t_all // tp,),
        in_specs=[
            pl.BlockSpec((tp, d), lambda i: (i, 0)),
            const2(g_col), const2(wq_t),
            pl.BlockSpec(keys.shape, lambda i: (0, 0, 0), pipeline_mode=pl.Buffered(1)),
        ],
        out_specs=[pl.BlockSpec((d, tp), lambda i: (0, i)), kspec, hspec, kspec, hspec,
                   pl.BlockSpec((PEER_HEADS, tp), lambda i: (0, i))],
        out_shape=[jax.ShapeDtypeStruct((d, t_all), BF16), kshape, hshape, kshape, hshape,
                   jax.ShapeDtypeStruct((PEER_HEADS, t_all), F32)],
        scratch_shapes=[
            pltpu.VMEM((2 * PEER_HEADS, PEER_NKEYS, tp), F32),
            pltpu.VMEM((2 * PEER_HEADS, PEER_TOPK, tp), F32),
            pltpu.VMEM((_CAND_ROWS, tp), F32),
            pltpu.VMEM((PEER_HEADS, PEER_NKEYS, tp), F32),
        ],
        compiler_params=_cparams(("parallel",)),
        name="peer_route",
    )(h1, g_col, wq_t, keys)


def _gelu_tanh(x):
    return 0.5 * x * (1.0 + jnp.tanh(0.7978845608028654 * (x + 0.044715 * (x * x * x))))


EXPERT_CHUNKS = 4
GATE_ROWS = 64


def _expert_kernel(xn_ref, u_ref, vt_ref, s1_ref, s2_ref, e1_ref, e2_ref, tau_ref, h_ref, o_ref,
                   acc_ref, wa_ref, wb_ref, *, tp, eb, n_blk):
    j = pl.program_id(1)
    n_i = eb // PEER_NKEYS
    n_lb = tp // LANES
    d_rows = acc_ref.shape[0] // EXPERT_CHUNKS
    e_rows = eb // EXPERT_CHUNKS

    @pl.when(j == 0)
    def _():
        acc_ref[...] = jnp.zeros_like(acc_ref)
        wb_ref[...] = jnp.zeros_like(wb_ref)

    def step(w_cur, w_prev):
        def chunk(c, _):
            dr = pl.ds(pl.multiple_of(c * d_rows, d_rows), d_rows)
            acc_ref[dr, :] += _dot(vt_ref[dr, :], w_prev[...])
            e0 = pl.multiple_of(c * e_rows, e_rows)
            act = _dot(u_ref[pl.ds(e0, e_rows), :], xn_ref[...])
            for t in range(e_rows // PEER_NKEYS):
                irow = j * n_i + c * (e_rows // PEER_NKEYS) + t
                for lb in range(n_lb):
                    lanes = slice(lb * LANES, (lb + 1) * LANES)
                    for part in range(PEER_NKEYS // GATE_ROWS):
                        keys = slice(part * GATE_ROWS, (part + 1) * GATE_ROWS)
                        gate = jnp.zeros((GATE_ROWS, LANES), F32)
                        for h in range(PEER_HEADS):
                            sel = (s1_ref[irow, h:h + 1, lanes] + s2_ref[h, keys, lanes]) >= tau_ref[h:h + 1, lanes]
                            gate = gate + jnp.where(sel, e1_ref[irow, h:h + 1, lanes] * e2_ref[h, keys, lanes], 0.0)
                        rows = slice(t * PEER_NKEYS + part * GATE_ROWS, t * PEER_NKEYS + (part + 1) * GATE_ROWS)
                        w_cur[pl.ds(e0 + rows.start, GATE_ROWS), lanes] = (
                            _gelu_tanh(act[rows, lanes]) * gate).astype(BF16)
            return 0

        lax.fori_loop(0, EXPERT_CHUNKS, chunk, 0)

    @pl.when(jnp.logical_and(j < n_blk, j % 2 == 0))
    def _():
        step(wa_ref, wb_ref)

    @pl.when(jnp.logical_and(j < n_blk, j % 2 == 1))
    def _():
        step(wb_ref, wa_ref)

    @pl.when(j == n_blk)
    def _():
        w_last = wb_ref if n_blk % 2 == 0 else wa_ref
        o_ref[...] = h_ref[...] + (acc_ref[...] + _dot(vt_ref[...], w_last[...])).T


def _experts(xn_t, u_b, vt_b, s1, s2, e1, e2, tau, h1, tp, eb):
    d, t_all = xn_t.shape
    n_blk = u_b.shape[0] // eb
    once = lambda shape, imap: pl.BlockSpec(shape, imap, pipeline_mode=pl.Buffered(1))
    hspec = once((PEER_HEADS, PEER_NKEYS, tp), lambda i, j: (0, 0, i))
    kspec = once((PEER_NKEYS, PEER_HEADS, tp), lambda i, j: (0, 0, i))
    return pl.pallas_call(
        functools.partial(_expert_kernel, tp=tp, eb=eb, n_blk=n_blk),
        grid=(t_all // tp, n_blk + 1),
        in_specs=[
            once((d, tp), lambda i, j: (0, i)),
            pl.BlockSpec((eb, d), lambda i, j: (jnp.minimum(j, n_blk - 1), 0)),
            pl.BlockSpec((d, eb), lambda i, j: (0, jnp.maximum(j - 1, 0))),
            kspec, hspec, kspec, hspec,
            once((PEER_HEADS, tp), lambda i, j: (0, i)),
            once((tp, d), lambda i, j: (i, 0)),
        ],
        out_specs=pl.BlockSpec((tp, d), lambda i, j: (i, 0)),
        out_shape=jax.ShapeDtypeStruct((t_all, d), F32),
        scratch_shapes=[
            pltpu.VMEM((d, tp), F32),
            pltpu.VMEM((eb, tp), BF16),
            pltpu.VMEM((eb, tp), BF16),
        ],
        compiler_params=_cparams(("parallel", "arbitrary")),
        name="peer_experts",
    )(xn_t, u_b, vt_b, s1, s2, e1, e2, tau, h1)


def _round_up(x, m):
    return -(-x // m) * m


def kernel(x_prompt, x_sample, cache_k, cache_v, cache_logf, state_conv, meta_tokens, w_in, g_mix, q_gain, k_gain, b_forget, conv_w, conv_b, g_conv_norm, w_conv_out, w_att_out, w_out, g_ffn, w_peer_q, peer_sub_keys, peer_u, peer_v):
    n_bp, seq, d = x_prompt.shape
    n_bs, t_new, _ = x_sample.shape
    depth, _, n_past = cache_k.shape[:3]
    assert n_bp == 1 and depth == 1 and t_new % 16 == 0 and t_new >= CONV_WIDTH - 1
    s_tot = N_META + seq
    tm = 512 if s_tot >= 2048 else 128
    t_p = _round_up(s_tot, tm)
    t_s = n_bs * t_new
    t_all = t_p + _round_up(t_s, tm)
    tc = 128
    eb = 8 * PEER_NKEYS

    h_all = jnp.concatenate([
        meta_tokens.astype(F32), x_prompt[0], jnp.zeros((t_p - s_tot, d), F32),
        x_sample.reshape(t_s, d), jnp.zeros((t_all - t_p - t_s, d), F32)], axis=0)

    w = w_in[0]
    o_q, o_k, o_v, o_f = 2 * D_CONV, 2 * D_CONV + D_ATTN, 2 * D_CONV + 2 * D_ATTN, 2 * D_CONV + 3 * D_ATTN
    o_g = o_f + N_HEADS
    w_main = jnp.concatenate([w[:, o_g:], w[:, :o_f]], axis=1).astype(BF16)
    w_f = jnp.pad(w[:, o_f:o_g], ((0, 0), (0, LANES - N_HEADS))).astype(BF16)
    col_a = 2 * d // D_CONV
    col_q = col_a + 2
    b_f = jnp.pad(b_forget[0], (0, LANES - N_HEADS)).reshape(1, LANES)

    z, zf = _inproj(h_all, g_mix[0].reshape(1, d), w_main, w_f, tm, 1024)
    q_s, k_f, v_f, k_b, v_b, v_t, logf = _qkv_post(z, zf, q_gain[0].reshape(1, D_ATTN), k_gain[0].reshape(1, D_ATTN),
                                              b_f, tm, col_q)

    cw = jnp.pad(conv_w[0], ((0, CONV_PAD - CONV_WIDTH), (0, 0)))
    cb = conv_b[0].reshape(1, D_CONV)
    gn = g_conv_norm[0].reshape(1, D_CONV)
    lead = CONV_PAD - (CONV_WIDTH - 1)
    yc_p, conv_p = _conv(z, jnp.zeros((1, CONV_PAD, D_CONV), F32), cw, cb, gn,
                         n_seq=1, seq_rows=t_p, row0=0, tc=tc, n_valid=s_tot, col_a=col_a)
    yc_s, conv_s = _conv(z, jnp.pad(state_conv[0], ((0, 0), (lead, 0), (0, 0))), cw, cb, gn,
                         n_seq=n_bs, seq_rows=t_new, row0=t_p, tc=t_new, n_valid=t_new, col_a=col_a)

    lf8 = logf[:, :N_HEADS]
    q_aug, k_aug = _attn_prep(q_s, k_b, _cumsum_rows(lf8[:t_p].T), t_p, tm)
    att_p = _flash_prompt(q_aug, k_aug, v_t, t_p, tm)
    l_pad = _round_up(n_past + t_new, LANES)
    lf_new = lf8[t_p:t_p + t_s].reshape(n_bs, t_new, N_HEADS)
    lf_all = jnp.concatenate([cache_logf[0].astype(F32), lf_new], axis=1)
    lf_all = jnp.pad(jnp.swapaxes(lf_all, 1, 2), ((0, 0), (0, 0), (0, l_pad - n_past - t_new)))
    c_s = _cumsum_rows(lf_all.reshape(n_bs * N_HEADS, l_pad)).reshape(n_bs * N_HEADS, 1, l_pad)
    ck = 1024 if n_past % 1024 == 0 else LANES
    att_s = _decode_attn(q_s, k_b, v_b, cache_k.reshape(n_bs, n_past * N_HEADS, HEAD_DIM),
                         cache_v.reshape(n_bs, n_past * N_HEADS, HEAD_DIM), c_s,
                         n_b=n_bs, t_new=t_new, n_past=n_past, row0=t_p, ck=ck)

    pad_s = t_all - t_p - t_s
    yc = jnp.concatenate([yc_p, yc_s, jnp.zeros((pad_s, D_CONV), BF16)], axis=0)
    att = jnp.concatenate([att_p, att_s, jnp.zeros((pad_s, D_ATTN), BF16)], axis=0)
    h1 = _merge(h_all, yc, att, z, w_conv_out[0].astype(BF16), w_att_out[0].astype(BF16),
                w_out[0].astype(BF16), min(tm, 256))

    keys = peer_sub_keys[0].reshape(2 * PEER_HEADS, PEER_NKEYS, PEER_NKEYS).astype(BF16)
    xn_t, s1, s2, e1, e2, tau = _route(h1, g_ffn[0].reshape(d, 1), w_peer_q[0].T.astype(BF16), keys, tm)
    y = _experts(xn_t, peer_u[0].astype(BF16), peer_v[0].T.astype(BF16), s1, s2, e1, e2, tau, h1, tm, eb)

    hd = (N_HEADS, HEAD_DIM)
    return (
        y[N_META:s_tot][None],
        y[t_p:t_p + t_s].reshape(n_bs, t_new, d),
        k_f[:s_tot].reshape(1, 1, s_tot, *hd),
        v_f[:s_tot].reshape(1, 1, s_tot, *hd),
        lf8[:s_tot].reshape(1, 1, s_tot, N_HEADS),
        conv_p[None],
        k_f[t_p:t_p + t_s].reshape(1, n_bs, t_new, *hd),
        v_f[t_p:t_p + t_s].reshape(1, n_bs, t_new, *hd),
        lf_new[None],
        conv_s[None],
    )
```
